```python
import math
import jax
import jax.numpy as jnp
from jax import lax
import numpy as np


D_MODEL = 2048
BATCH = 4
SEQ = 4096
DEPTH = 2

N_MIXERS = 2
N_S5_LAYERS = (DEPTH + N_MIXERS - 1) // N_MIXERS
N_ATTN_LAYERS = DEPTH // N_MIXERS
S5_GROUP = 16
S5_GROUPS = D_MODEL // S5_GROUP
S5_STATE = 64
S5_DT_MIN = 1e-3
S5_DT_MAX = 1e-1
N_HEADS = 16
HEAD_DIM = D_MODEL // N_HEADS
MOBA_BLOCK = 256
MOBA_TOPK = 3
MOBA_QCHUNK = 16
REL_BUCKETS = 32
REL_MAX_DIST = 128
FFN_HIDDEN = (8 * D_MODEL + 3 * 256 - 1) // (3 * 256) * 256
RMS_EPS = 1e-6
NEG_INF = -1e30

kernel_name = 'hybrid_s5_moba_swiglu'


def rmsnorm(x, g):
    xf = x.astype(jnp.float32)
    y = xf * lax.rsqrt(jnp.mean(xf * xf, axis=-1, keepdims=True) + RMS_EPS)
    return (y * g.astype(jnp.float32)).astype(x.dtype)


def swiglu_ffn(u, w_in, w_out):
    gate, up = jnp.split(u @ w_in, 2, axis=-1)
    return (jax.nn.silu(gate) * up) @ w_out


def t5_bucket(rel):
    n = jnp.maximum(rel, 0)
    max_exact = REL_BUCKETS // 2
    nf = jnp.maximum(n, 1).astype(jnp.float32)
    large = max_exact + (jnp.log(nf / max_exact) / math.log(REL_MAX_DIST / max_exact)
                         * (REL_BUCKETS - max_exact)).astype(jnp.int32)
    large = jnp.minimum(large, REL_BUCKETS - 1)
    return jnp.where(n < max_exact, n, large)


def s5_mixer(u, a_re, a_im, log_step, b_re, b_im, c_re, c_im, d_skip, w_glu):
    bsz, seq, _ = u.shape
    f32 = jnp.float32
    ug = u.astype(f32).reshape(bsz, seq, S5_GROUPS, S5_GROUP)
    lam = lax.complex(a_re.astype(f32), a_im.astype(f32))
    step = jnp.exp(log_step.astype(f32))[:, None]
    a_bar = jnp.exp(lam * step)
    b = lax.complex(b_re.astype(f32), b_im.astype(f32))
    b_bar = ((a_bar - 1.0) / lam)[..., None] * b
    c = lax.complex(c_re.astype(f32), c_im.astype(f32))
    bu = jnp.einsum('blgh,gph->blgp', ug.astype(jnp.complex64), b_bar)
    a_seq = jnp.broadcast_to(a_bar, (1, seq) + a_bar.shape)

    def combine(e1, e2):
        a1, s1 = e1
        a2, s2 = e2
        return a2 * a1, a2 * s1 + s2

    _, states = lax.associative_scan(combine, (a_seq, bu), axis=1)
    y = jnp.real(jnp.einsum('blgp,ghp->blgh', states, c)) \
        + d_skip.astype(f32).reshape(S5_GROUPS, S5_GROUP) * ug
    z = jax.nn.gelu(y.reshape(bsz, seq, D_MODEL)).astype(u.dtype)
    val, gate = jnp.split(z @ w_glu, 2, axis=-1)
    return val * jax.nn.sigmoid(gate)


def moba_mixer(u, w_qkv, w_o, rel_bias):
    bsz, seq, _ = u.shape
    f32 = jnp.float32
    n_blk = -(-seq // MOBA_BLOCK)
    seq_p = n_blk * MOBA_BLOCK
    n_chunk = seq_p // MOBA_QCHUNK
    k_sel = min(MOBA_TOPK, n_blk)
    qkv = jnp.pad((u @ w_qkv).astype(f32), ((0, 0), (0, seq_p - seq), (0, 0)))
    qkv = qkv.reshape(bsz, seq_p, 3, N_HEADS, HEAD_DIM).transpose(2, 0, 3, 1, 4)
    q = qkv[0] * (HEAD_DIM ** -0.5)
    kb = qkv[1].reshape(bsz, N_HEADS, n_blk, MOBA_BLOCK, HEAD_DIM)
    vb = qkv[2].reshape(bsz, N_HEADS, n_blk, MOBA_BLOCK, HEAD_DIM)
    bias_tab = rel_bias.astype(f32)

    k_mean = jnp.mean(kb, axis=3)
    gate = jnp.einsum('bhtd,bhnd->bhtn', q, k_mean)
    pos = jnp.arange(seq_p)
    past = jnp.arange(n_blk)[None, :] < (pos // MOBA_BLOCK)[:, None]
    past = jnp.broadcast_to(past, gate.shape)
    _, sel = lax.top_k(jnp.where(past, gate, -jnp.inf), k_sel)
    sel_valid = jnp.take_along_axis(past, sel, axis=-1)

    def to_chunks(a):
        return jnp.moveaxis(a.reshape(bsz, N_HEADS, n_chunk, MOBA_QCHUNK, a.shape[-1]), 2, 0)

    gather_blocks = jax.vmap(jax.vmap(lambda blocks, idx: blocks[idx]))
    head_idx = jnp.arange(N_HEADS)[None, :, None, None, None]
    blk_off = jnp.arange(MOBA_BLOCK)

    def attend_chunk(args):
        qc, selc, validc, ci = args
        t = ci * MOBA_QCHUNK + jnp.arange(MOBA_QCHUNK)
        kg = gather_blocks(kb, selc)
        vg = gather_blocks(vb, selc)
        key_pos = selc[..., None] * MOBA_BLOCK + blk_off
        s_sel = jnp.einsum('bhqd,bhqnkd->bhqnk', qc, kg) \
            + bias_tab[t5_bucket(t[None, None, :, None, None] - key_pos), head_idx]
        s_sel = jnp.where(validc[..., None], s_sel, NEG_INF)
        blk = (ci * MOBA_QCHUNK) // MOBA_BLOCK
        k_own = lax.dynamic_index_in_dim(kb, blk, axis=2, keepdims=False)
        v_own = lax.dynamic_index_in_dim(vb, blk, axis=2, keepdims=False)
        rel = t[:, None] - (blk * MOBA_BLOCK + blk_off)[None, :]
        s_own = jnp.einsum('bhqd,bhkd->bhqk', qc, k_own) \
            + jnp.transpose(bias_tab[t5_bucket(rel)], (2, 0, 1))[None]
        s_own = jnp.where(rel >= 0, s_own, NEG_INF)
        logits = jnp.concatenate(
            [s_sel.reshape(bsz, N_HEADS, MOBA_QCHUNK, k_sel * MOBA_BLOCK), s_own], axis=-1)
        p = jax.nn.softmax(logits, axis=-1)
        p_sel = p[..., :k_sel * MOBA_BLOCK].reshape(bsz, N_HEADS, MOBA_QCHUNK, k_sel, MOBA_BLOCK)
        return jnp.einsum('bhqnk,bhqnkd->bhqd', p_sel, vg) \
            + jnp.einsum('bhqk,bhkd->bhqd', p[..., k_sel * MOBA_BLOCK:], v_own)

    out = lax.map(attend_chunk, (to_chunks(q), to_chunks(sel), to_chunks(sel_valid),
                                 jnp.arange(n_chunk)))
    out = jnp.moveaxis(out, 0, 2).reshape(bsz, N_HEADS, seq_p, HEAD_DIM)[:, :, :seq]
    out = out.transpose(0, 2, 1, 3).reshape(bsz, seq, D_MODEL).astype(u.dtype)
    return out @ w_o


def setup_inputs(seed: int = 0) -> dict:
    key = jax.random.key(seed)
    ks = jax.random.split(key, 18)
    f32 = jnp.float32
    nrm = lambda k, shape, s: jax.random.normal(k, shape, f32) * s
    x = jax.random.normal(ks[0], (BATCH, SEQ, D_MODEL), f32)
    norm_mix_g = 1.0 + nrm(ks[1], (DEPTH, D_MODEL), 0.01)
    norm_ffn_g = 1.0 + nrm(ks[2], (DEPTH, D_MODEL), 0.01)
    norm_final_g = 1.0 + nrm(ks[3], (D_MODEL,), 0.01)
    shp_gp = (N_S5_LAYERS, S5_GROUPS, S5_STATE)
    s5_a_re = -0.5 + nrm(ks[4], shp_gp, 0.01)
    s5_a_im = jnp.pi * jnp.arange(S5_STATE, dtype=f32) + nrm(ks[5], shp_gp, 0.01)
    s5_log_step = jax.random.uniform(ks[6], (N_S5_LAYERS, S5_GROUPS), f32,
                                     math.log(S5_DT_MIN), math.log(S5_DT_MAX))
    shp_b = (N_S5_LAYERS, S5_GROUPS, S5_STATE, S5_GROUP)
    s5_b_re = nrm(ks[7], shp_b, (2 * S5_GROUP) ** -0.5)
    s5_b_im = nrm(ks[8], shp_b, (2 * S5_GROUP) ** -0.5)
    shp_c = (N_S5_LAYERS, S5_GROUPS, S5_GROUP, S5_STATE)
    s5_c_re = nrm(ks[9], shp_c, S5_STATE ** -0.5)
    s5_c_im = nrm(ks[10], shp_c, S5_STATE ** -0.5)
    s5_d = nrm(ks[11], (N_S5_LAYERS, D_MODEL), 1.0)
    s5_w_glu = nrm(ks[12], (N_S5_LAYERS, D_MODEL, 2 * D_MODEL), D_MODEL ** -0.5)
    attn_w_qkv = nrm(ks[13], (N_ATTN_LAYERS, D_MODEL, 3 * D_MODEL), D_MODEL ** -0.5)
    attn_w_o = nrm(ks[14], (N_ATTN_LAYERS, D_MODEL, D_MODEL), D_MODEL ** -0.5)
    rel_bias = nrm(ks[15], (REL_BUCKETS, N_HEADS), 0.2)
    ffn_w_in = nrm(ks[16], (DEPTH, D_MODEL, 2 * FFN_HIDDEN), D_MODEL ** -0.5)
    ffn_w_out = nrm(ks[17], (DEPTH, FFN_HIDDEN, D_MODEL), FFN_HIDDEN ** -0.5)
    return {'x': x, 'norm_mix_g': norm_mix_g, 'norm_ffn_g': norm_ffn_g,
            'norm_final_g': norm_final_g, 's5_a_re': s5_a_re, 's5_a_im': s5_a_im,
            's5_log_step': s5_log_step, 's5_b_re': s5_b_re, 's5_b_im': s5_b_im,
            's5_c_re': s5_c_re, 's5_c_im': s5_c_im, 's5_d': s5_d, 's5_w_glu': s5_w_glu,
            'attn_w_qkv': attn_w_qkv, 'attn_w_o': attn_w_o, 'rel_bias': rel_bias,
            'ffn_w_in': ffn_w_in, 'ffn_w_out': ffn_w_out}


def reference(x, norm_mix_g, norm_ffn_g, norm_final_g, s5_a_re, s5_a_im, s5_log_step,
              s5_b_re, s5_b_im, s5_c_re, s5_c_im, s5_d, s5_w_glu, attn_w_qkv, attn_w_o,
              rel_bias, ffn_w_in, ffn_w_out):
    h = x
    for layer in range(DEPTH):
        u = rmsnorm(h, norm_mix_g[layer])
        j = layer // N_MIXERS
        if layer % N_MIXERS == 0:
            mix = s5_mixer(u, s5_a_re[j], s5_a_im[j], s5_log_step[j], s5_b_re[j], s5_b_im[j],
                           s5_c_re[j], s5_c_im[j], s5_d[j], s5_w_glu[j])
        else:
            mix = moba_mixer(u, attn_w_qkv[j], attn_w_o[j], rel_bias)
        h = h + mix.astype(h.dtype)
        h = h + swiglu_ffn(rmsnorm(h, norm_ffn_g[layer]), ffn_w_in[layer], ffn_w_out[layer]).astype(h.dtype)
    return rmsnorm(h, norm_final_g)
```

```python
import functools
import math

import numpy as np
import jax
import jax.numpy as jnp
from jax import lax
from jax.experimental import pallas as pl
from jax.experimental.pallas import tpu as pltpu

F32 = jnp.float32
BF16 = jnp.bfloat16

RMS_EPS = 1e-6
S5_GROUP = 16
S5_CHUNK = 16
HEAD_DIM = 128
MOBA_BLOCK = 256
MOBA_TOPK = 3
REL_BUCKETS = 32
REL_MAX_DIST = 128
NEG_INF = -1e30
LANES = 128
VMEM_LIMIT = 56 * 1024 * 1024


def _pick(n, pref):
    if n <= pref:
        return n
    for t in range(pref, 7, -1):
        if n % t == 0 and t % 8 == 0:
            return t
    return n


def _params(sem):
    return pltpu.CompilerParams(dimension_semantics=sem, vmem_limit_bytes=VMEM_LIMIT)


def _rmsnorm_body(x_ref, g_ref, o_ref):
    x = x_ref[...]
    ms = jnp.mean(x * x, axis=-1, keepdims=True)
    o_ref[...] = (x * lax.rsqrt(ms + RMS_EPS) * g_ref[...]).astype(o_ref.dtype)


def _rmsnorm(x, g, out_dtype):
    t, d = x.shape
    tm = _pick(t, 512)
    return pl.pallas_call(
        _rmsnorm_body,
        grid=(t // tm,),
        in_specs=[pl.BlockSpec((tm, d), lambda i: (i, 0)),
                  pl.BlockSpec((1, d), lambda i: (0, 0))],
        out_specs=pl.BlockSpec((tm, d), lambda i: (i, 0)),
        out_shape=jax.ShapeDtypeStruct((t, d), out_dtype),
        compiler_params=_params(("arbitrary",)),
        name="rmsnorm",
    )(x, g.reshape(1, d).astype(F32))


def _mm_pair_body(a_ref, w0_ref, w1_ref, *rest, kind):
    if kind == "glu":
        h_ref, o_ref, w0q, w1q = rest
    else:
        o_ref, w0q, w1q = rest

    @pl.when(pl.program_id(1) == 0)
    def _():
        w0q[...] = w0_ref[...].astype(BF16)
        w1q[...] = w1_ref[...].astype(BF16)

    a = a_ref[...]
    y0 = jnp.dot(a, w0q[...], preferred_element_type=F32)
    y1 = jnp.dot(a, w1q[...], preferred_element_type=F32)
    if kind == "glu":
        o_ref[...] = h_ref[...] + y0 * jax.nn.sigmoid(y1)
    else:
        o_ref[...] = (y0 * jax.nn.sigmoid(y0) * y1).astype(o_ref.dtype)


def _mm_pair(a, w, layer, kind, h=None, tm_pref=1024, tn_pref=512):
    t, k = a.shape
    n = w.shape[2] // 2
    tm, tn = _pick(t, tm_pref), _pick(n, tn_pref)
    nj = n // tn
    in_specs = [pl.BlockSpec((tm, k), lambda j, i: (i, 0)),
                pl.BlockSpec((None, k, tn), lambda j, i: (layer, 0, j)),
                pl.BlockSpec((None, k, tn), lambda j, i: (layer, 0, j + nj))]
    args = [a, w, w]
    if kind == "glu":
        in_specs.append(pl.BlockSpec((tm, tn), lambda j, i: (i, j)))
        args.append(h)
        out_dtype = F32
    else:
        out_dtype = BF16
    return pl.pallas_call(
        functools.partial(_mm_pair_body, kind=kind),
        grid=(nj, t // tm),
        in_specs=in_specs,
        out_specs=pl.BlockSpec((tm, tn), lambda j, i: (i, j)),
        out_shape=jax.ShapeDtypeStruct((t, n), out_dtype),
        scratch_shapes=[pltpu.VMEM((k, tn), BF16), pltpu.VMEM((k, tn), BF16)],
        compiler_params=_params(("arbitrary", "arbitrary")),
        name="mm_" + kind,
    )(*args)


def _mm_single_body(a_ref, w_ref, *rest, kind, n_scaled, scale):
    if kind == "resid":
        h_ref, o_ref, wq = rest
    else:
        o_ref, wq = rest

    @pl.when(pl.program_id(1) == 0)
    def _():
        wq[...] = w_ref[...].astype(BF16)

    y = jnp.dot(a_ref[...], wq[...], preferred_element_type=F32)
    if kind == "resid":
        o_ref[...] = h_ref[...] + y
    else:
        s = jnp.where(pl.program_id(0) < n_scaled, scale, 1.0).astype(F32)
        o_ref[...] = (y * s).astype(o_ref.dtype)


def _mm_single(a, w, layer, kind, h=None, tm_pref=1024, tn_pref=512, n_scaled_cols=0, scale=1.0):
    t, k = a.shape
    n = w.shape[2]
    tm, tn = _pick(t, tm_pref), _pick(n, tn_pref)
    in_specs = [pl.BlockSpec((tm, k), lambda j, i: (i, 0)),
                pl.BlockSpec((None, k, tn), lambda j, i: (layer, 0, j))]
    args = [a, w]
    if kind == "resid":
        in_specs.append(pl.BlockSpec((tm, tn), lambda j, i: (i, j)))
        args.append(h)
        out_dtype = F32
    else:
        assert n_scaled_cols % tn == 0
        out_dtype = BF16
    return pl.pallas_call(
        functools.partial(_mm_single_body, kind=kind, n_scaled=n_scaled_cols // tn, scale=scale),
        grid=(n // tn, t // tm),
        in_specs=in_specs,
        out_specs=pl.BlockSpec((tm, tn), lambda j, i: (i, j)),
        out_shape=jax.ShapeDtypeStruct((t, n), out_dtype),
        scratch_shapes=[pltpu.VMEM((k, tn), BF16)],
        compiler_params=_params(("arbitrary", "arbitrary")),
        name="mm_" + kind,
    )(*args)


def _s5_body(u_ref, m_ref, bb_ref, cc_ref, pw_ref, o_ref, *, chunks, nlev):
    u = u_ref[0]
    near = jnp.dot(u, m_ref[0], preferred_element_type=F32)
    x = jnp.dot(u, bb_ref[0], preferred_element_type=F32)
    half = x.shape[1] // 2
    row = lax.broadcasted_iota(jnp.int32, x.shape, 0) % chunks
    for lev in range(nlev):
        sh = 1 << lev
        xs = jnp.where(row >= sh, pltpu.roll(x, sh, axis=0), 0.0)
        p1 = pw_ref[0, 2 * lev:2 * lev + 1, :]
        p2 = pw_ref[0, 2 * lev + 1:2 * lev + 2, :]
        x = x + xs * p1 + pltpu.roll(xs, half, axis=1) * p2
    xprev = jnp.where(row >= 1, pltpu.roll(x, 1, axis=0), 0.0)
    far = jnp.dot(xprev.astype(BF16), cc_ref[0], preferred_element_type=F32)
    o_ref[0] = jax.nn.gelu(near + far).astype(o_ref.dtype)


def _s5_weights(a_re, a_im, log_step, b_re, b_im, c_re, c_im, d_skip, chunks):
    g, p = a_re.shape
    hh = b_re.shape[-1]
    tc = S5_CHUNK
    hi = lax.Precision.HIGHEST
    step = jnp.exp(log_step.astype(F32))[:, None]
    lr, li = a_re.astype(F32) * step, a_im.astype(F32) * step

    def apow(tau):
        tau = jnp.asarray(tau, F32)[None, :, None]
        mag = jnp.exp(lr[:, None, :] * tau)
        return mag * jnp.cos(li[:, None, :] * tau), mag * jnp.sin(li[:, None, :] * tau)

    ar, ai = jnp.exp(lr) * jnp.cos(li), jnp.exp(lr) * jnp.sin(li)
    den = a_re * a_re + a_im * a_im
    cf_re = ((ar - 1.0) * a_re + ai * a_im) / den
    cf_im = (ai * a_re - (ar - 1.0) * a_im) / den
    bb_re = cf_re[..., None] * b_re - cf_im[..., None] * b_im
    bb_im = cf_re[..., None] * b_im + cf_im[..., None] * b_re

    pr, pi = apow(np.arange(tc + 1))
    w_re = pr[:, :tc, :, None] * bb_re[:, None] - pi[:, :tc, :, None] * bb_im[:, None]
    w_im = pr[:, :tc, :, None] * bb_im[:, None] + pi[:, :tc, :, None] * bb_re[:, None]
    kern = (jnp.einsum("gkp,gtph->gtkh", c_re, w_re, precision=hi)
            - jnp.einsum("gkp,gtph->gtkh", c_im, w_im, precision=hi))

    s_idx = np.arange(tc)[:, None]
    t_idx = np.arange(tc)[None, :]
    lag = np.clip(t_idx - s_idx, 0, tc - 1)
    causal = jnp.asarray((t_idx >= s_idx), F32)
    m = kern[:, lag] * causal[None, :, :, None, None]
    m = m.transpose(0, 1, 4, 2, 3).reshape(g, tc * hh, tc * hh)
    m = m + jnp.eye(tc * hh, dtype=F32)[None] * jnp.tile(d_skip.reshape(g, hh), (1, tc))[:, None, :]

    rev = np.arange(tc - 1, -1, -1)
    bmat = jnp.concatenate([w_re[:, rev].transpose(0, 1, 3, 2), w_im[:, rev].transpose(0, 1, 3, 2)],
                           axis=-1).reshape(g, tc * hh, 2 * p)

    qr, qi = pr[:, 1:], pi[:, 1:]
    cp_re = c_re[:, None] * qr[:, :, None, :] - c_im[:, None] * qi[:, :, None, :]
    cp_im = c_re[:, None] * qi[:, :, None, :] + c_im[:, None] * qr[:, :, None, :]
    cmat = jnp.concatenate([cp_re.transpose(0, 3, 1, 2).reshape(g, p, tc * hh),
                            -cp_im.transpose(0, 3, 1, 2).reshape(g, p, tc * hh)], axis=1)

    nlev = max(1, int(math.ceil(math.log2(chunks))))
    sr, si = apow(tc * (2 ** np.arange(nlev)))
    pw = jnp.stack([jnp.concatenate([sr, sr], -1), jnp.concatenate([-si, si], -1)], axis=2)
    pw = pw.reshape(g, 2 * nlev, 2 * p)
    pad = (-2 * nlev) % 8
    pw = jnp.pad(pw, ((0, 0), (0, pad), (0, 0)))
    return m.astype(BF16), bmat.astype(BF16), cmat.astype(BF16), pw, nlev


def _s5_mixer(u, bsz, seq, a_re, a_im, log_step, b_re, b_im, c_re, c_im, d_skip):
    t, d = u.shape
    g = d // S5_GROUP
    tc = S5_CHUNK
    assert seq % tc == 0
    chunks = seq // tc
    rows = bsz * chunks
    wide = tc * S5_GROUP
    m, bmat, cmat, pw, nlev = _s5_weights(a_re, a_im, log_step, b_re, b_im, c_re, c_im, d_skip, chunks)
    ug = u.reshape(bsz, chunks, tc, g, S5_GROUP).transpose(3, 0, 1, 2, 4).reshape(g, rows, wide)
    p2 = bmat.shape[-1]
    z = pl.pallas_call(
        functools.partial(_s5_body, chunks=chunks, nlev=nlev),
        grid=(g,),
        in_specs=[pl.BlockSpec((1, rows, wide), lambda i: (i, 0, 0)),
                  pl.BlockSpec((1, wide, wide), lambda i: (i, 0, 0)),
                  pl.BlockSpec((1, wide, p2), lambda i: (i, 0, 0)),
                  pl.BlockSpec((1, p2, wide), lambda i: (i, 0, 0)),
                  pl.BlockSpec((1, pw.shape[1], p2), lambda i: (i, 0, 0))],
        out_specs=pl.BlockSpec((1, rows, wide), lambda i: (i, 0, 0)),
        out_shape=jax.ShapeDtypeStruct((g, rows, wide), BF16),
        compiler_params=_params(("arbitrary",)),
        name="s5_chunked",
    )(ug, m, bmat, cmat, pw)
    return z.reshape(g, bsz, chunks, tc, S5_GROUP).transpose(1, 2, 3, 0, 4).reshape(t, d)


def _t5_bucket(rel):
    n = jnp.maximum(rel, 0)
    max_exact = REL_BUCKETS // 2
    nf = jnp.maximum(n, 1).astype(F32)
    large = max_exact + (jnp.log(nf / max_exact) / math.log(REL_MAX_DIST / max_exact)
                         * (REL_BUCKETS - max_exact)).astype(jnp.int32)
    large = jnp.minimum(large, REL_BUCKETS - 1)
    return jnp.where(n < max_exact, n, large)


def _moba_body(q_ref, k_ref, v_ref, bo_ref, bp_ref, o_ref,
               kmean_ref, sb_ref, m_ref, l_ref, acc_ref, *, nb):
    i = pl.program_id(2)
    bs = MOBA_BLOCK
    nt = (((1,), (1,)), ((), ()))

    @pl.when(i == 0)
    def _():
        kmean_ref[...] = jnp.zeros_like(kmean_ref)
        kf = k_ref[...].astype(F32).reshape(nb, bs, HEAD_DIM)
        kmean_ref[0:nb, :] = jnp.mean(kf, axis=1)

    q = q_ref[...]

    gate = lax.dot_general(q.astype(F32), kmean_ref[...], nt, precision=lax.Precision.HIGHEST,
                           preferred_element_type=F32)
    col = lax.broadcasted_iota(jnp.int32, gate.shape, 1)
    gate = jnp.where(col < i, gate, -jnp.inf)
    sel = jnp.zeros(gate.shape, jnp.bool_)
    for _ in range(MOBA_TOPK):
        best = jnp.max(gate, axis=1, keepdims=True)
        idx = jnp.min(jnp.where(gate == best, col, LANES), axis=1, keepdims=True)
        pick = (col == idx) & (best > -jnp.inf)
        sel = sel | pick
        gate = jnp.where(pick, -jnp.inf, gate)
    selbias = jnp.where(sel, 0.0, NEG_INF).astype(F32)
    for n in range(nb - 1):
        @pl.when(n < i)
        def _():
            sb_ref[n] = jnp.broadcast_to(selbias[:, n:n + 1], (bs, LANES))

    def kv_block(n):
        rows = pl.ds(pl.multiple_of(n * bs, bs), bs)
        return k_ref[rows, :], v_ref[rows, :]

    def update(s, vblk):
        m_prev = m_ref[...]
        m_new = jnp.maximum(m_prev, jnp.max(s, axis=1, keepdims=True))
        alpha = jnp.exp(m_prev - m_new)
        p = jnp.exp(s - m_new)
        l_ref[...] = alpha * l_ref[...] + jnp.sum(p, axis=1, keepdims=True)
        acc_ref[...] = alpha * acc_ref[...] + jnp.dot(p.astype(BF16), vblk, preferred_element_type=F32)
        m_ref[...] = m_new

    kblk, vblk = kv_block(i)
    s = lax.dot_general(q, kblk, nt, preferred_element_type=F32) + bo_ref[0]
    m0 = jnp.max(s, axis=1, keepdims=True)
    p = jnp.exp(s - m0)
    m_ref[...] = m0
    l_ref[...] = jnp.sum(p, axis=1, keepdims=True)
    acc_ref[...] = jnp.dot(p.astype(BF16), vblk, preferred_element_type=F32)

    @pl.when(i >= 1)
    def _():
        kblk, vblk = kv_block(i - 1)
        sb = sb_ref[i - 1]
        s = (lax.dot_general(q, kblk, nt, preferred_element_type=F32) + bp_ref[0]
             + jnp.concatenate([sb, sb], axis=1))
        update(s, vblk)

    def far_block(n, carry):
        kblk, vblk = kv_block(n)
        sb = sb_ref[n]
        s = lax.dot_general(q, kblk, nt, preferred_element_type=F32) + jnp.concatenate([sb, sb], axis=1)
        update(s, vblk)
        return carry

    lax.fori_loop(0, jnp.maximum(i - 1, 0), far_block, 0)
    o_ref[...] = (acc_ref[...] / l_ref[...]).astype(o_ref.dtype)


def _moba_attention(qkv, bsz, seq, rel_bias):
    t, d3 = qkv.shape
    d = d3 // 3
    heads = d // HEAD_DIM
    bs = MOBA_BLOCK
    assert seq % bs == 0
    nb = seq // bs
    assert nb <= LANES

    far = np.arange(bs + 1, max(seq, bs + 2))
    nf = np.maximum(far, 1).astype(np.float64)
    far_bucket = REL_BUCKETS // 2 + (np.log(nf / (REL_BUCKETS // 2)) / math.log(REL_MAX_DIST / (REL_BUCKETS // 2))
                                     * (REL_BUCKETS - REL_BUCKETS // 2)).astype(np.int64)
    assert far_bucket.min() >= REL_BUCKETS - 1
    tab = rel_bias.astype(F32)
    tab = tab - tab[REL_BUCKETS - 1:REL_BUCKETS]
    pos = jnp.arange(bs)
    rel = pos[:, None] - pos[None, :]
    b_own = jnp.where((rel >= 0)[..., None], tab[_t5_bucket(rel)], NEG_INF).transpose(2, 0, 1)
    b_prev = tab[_t5_bucket(rel + bs)].transpose(2, 0, 1)

    return pl.pallas_call(
        functools.partial(_moba_body, nb=nb),
        grid=(bsz, heads, nb),
        in_specs=[pl.BlockSpec((bs, HEAD_DIM), lambda b, h, i: (b * nb + i, h)),
                  pl.BlockSpec((seq, HEAD_DIM), lambda b, h, i: (b, heads + h)),
                  pl.BlockSpec((seq, HEAD_DIM), lambda b, h, i: (b, 2 * heads + h)),
                  pl.BlockSpec((1, bs, bs), lambda b, h, i: (h, 0, 0)),
                  pl.BlockSpec((1, bs, bs), lambda b, h, i: (h, 0, 0))],
        out_specs=pl.BlockSpec((bs, HEAD_DIM), lambda b, h, i: (b * nb + i, h)),
        out_shape=jax.ShapeDtypeStruct((t, d), BF16),
        scratch_shapes=[pltpu.VMEM((LANES, HEAD_DIM), F32),
                        pltpu.VMEM((nb, bs, LANES), F32),
                        pltpu.VMEM((bs, 1), F32),
                        pltpu.VMEM((bs, 1), F32),
                        pltpu.VMEM((bs, HEAD_DIM), F32)],
        compiler_params=_params(("arbitrary", "arbitrary", "arbitrary")),
        name="moba_attention",
    )(qkv, qkv, qkv, b_own, b_prev)


def _ffn(h, g, w_in, w_out, layer):
    u = _rmsnorm(h, g, BF16)
    act = _mm_pair(u, w_in, layer, "swiglu")
    return _mm_single(act, w_out, layer, "resid", h=h, tm_pref=512, tn_pref=512)


def kernel(x, norm_mix_g, norm_ffn_g, norm_final_g, s5_a_re, s5_a_im, s5_log_step, s5_b_re, s5_b_im, s5_c_re, s5_c_im, s5_d, s5_w_glu, attn_w_qkv, attn_w_o, rel_bias, ffn_w_in, ffn_w_out):
    bsz, seq, d = x.shape
    depth = norm_mix_g.shape[0]
    h = x.reshape(bsz * seq, d).astype(F32)
    for layer in range(depth):
        u = _rmsnorm(h, norm_mix_g[layer], BF16)
        j = layer // 2
        if layer % 2 == 0:
            z = _s5_mixer(u, bsz, seq, s5_a_re[j], s5_a_im[j], s5_log_step[j], s5_b_re[j], s5_b_im[j],
                          s5_c_re[j], s5_c_im[j], s5_d[j])
            h = _mm_pair(z, s5_w_glu, j, "glu", h=h)
        else:
            qkv = _mm_single(u, attn_w_qkv, j, "qkv", tn_pref=_pick(d, 512), n_scaled_cols=d,
                             scale=HEAD_DIM ** -0.5)
            att = _moba_attention(qkv, bsz, seq, rel_bias)
            h = _mm_single(att, attn_w_o, j, "resid", h=h)
        h = _ffn(h, norm_ffn_g[layer], ffn_w_in, ffn_w_out, layer)
    return _rmsnorm(h, norm_final_g, x.dtype).reshape(bsz, seq, d)
```
